```python
import math
import jax, jax.numpy as jnp
from jax import lax
import numpy as np


D_MODEL = 2048
BATCH = 2
SEQ = 4096
DEPTH = 1

GRID_W = 64
CTX_LEN = 256
CHUNK = 128
A_HEADS = 8
A_DIM = 128
A_WIDTH = A_HEADS * A_DIM
R_HEADS = 8
R_DK = 128
R_DV = 128
R_WIDTH = R_HEADS * R_DV
MIX_WIDTH = A_WIDTH + R_WIDTH
IN_COLS = 2 * A_WIDTH + 2 * R_HEADS * R_DK + 2 * R_WIDTH
SPLITS = [A_WIDTH, 2 * A_WIDTH, 2 * A_WIDTH + R_HEADS * R_DK, 2 * A_WIDTH + 2 * R_HEADS * R_DK,
          2 * A_WIDTH + 2 * R_HEADS * R_DK + R_WIDTH]
ROPE_BASE = 10000.0
N_EXPERTS = 64
TOP_K = 8
N_GROUPS = 8
TOPK_GROUPS = 4
EXPERT_HIDDEN = 512
SHARED_HIDDEN = 512
ROUTED_SCALE = 2.5
MOE_BLOCK = 128
EPS = 1e-6

kernel_name = 'hybrid_gmlp_retention_moe_dit'


def rmsnorm(x, g):
    x32 = x.astype(jnp.float32)
    y = x32 * lax.rsqrt(jnp.mean(x32 * x32, axis=-1, keepdims=True) + EPS)
    return y.astype(x.dtype) * g


def layernorm(x, g):
    x32 = x.astype(jnp.float32)
    mu = jnp.mean(x32, axis=-1, keepdims=True)
    var = jnp.mean(jnp.square(x32 - mu), axis=-1, keepdims=True)
    return ((x32 - mu) * lax.rsqrt(var + EPS)).astype(x.dtype) * g


def head_groupnorm(o, g):
    B_, T, H, dv = o.shape
    o32 = o.astype(jnp.float32)
    mu = jnp.mean(o32, axis=-1, keepdims=True)
    var = jnp.mean(jnp.square(o32 - mu), axis=-1, keepdims=True)
    y = (o32 - mu) * lax.rsqrt(var + EPS)
    return y.reshape(B_, T, H * dv).astype(o.dtype) * g


def modulate(h, shift, scale):
    return h * (1.0 + scale) + shift


def apply_rope(x, cos, sin):
    half = x.shape[-1] // 2
    x1, x2 = x[..., :half], x[..., half:]
    return jnp.concatenate([x1 * cos - x2 * sin, x1 * sin + x2 * cos], axis=-1).astype(x.dtype)


def chunk_mlp(u, v, ln_g, w_s, b_s):
    B_, T, _ = u.shape
    nc = T // CHUNK
    u = jax.nn.gelu(u)
    v = layernorm(jax.nn.gelu(v), ln_g)
    v = v.reshape(B_, nc, CHUNK, A_HEADS, A_DIM)
    s = jnp.einsum('hpq,bcqhd->bcphd', w_s, v) + b_s.T[None, None, :, :, None]
    return u * s.reshape(B_, T, A_WIDTH)


def retention_scan(q, k, v, log_gamma, s0):
    B_, T, H, dk = q.shape
    dv = v.shape[-1]
    nc = T // CHUNK
    f32 = jnp.float32
    qc = q.astype(f32).reshape(B_, nc, CHUNK, H, dk).transpose(1, 0, 3, 2, 4)
    kc = k.astype(f32).reshape(B_, nc, CHUNK, H, dk).transpose(1, 0, 3, 2, 4)
    vc = v.astype(f32).reshape(B_, nc, CHUNK, H, dv).transpose(1, 0, 3, 2, 4)
    lg = log_gamma.astype(f32)[:, None]
    pos = jnp.arange(CHUNK, dtype=f32)
    diff = pos[:, None] - pos[None, :]
    d_in = jnp.where(diff[None] >= 0, jnp.exp(lg[:, :, None] * jnp.maximum(diff, 0.0)[None]), 0.0)
    q_decay = jnp.exp(lg * (pos + 1.0))
    k_decay = jnp.exp(lg * (CHUNK - 1.0 - pos))
    c_decay = jnp.exp(lg[:, 0] * CHUNK)

    def step(S, inp):
        qi, ki, vi = inp
        scores = jnp.einsum('bhnd,bhmd->bhnm', qi, ki) * d_in[None]
        inner = jnp.einsum('bhnm,bhmv->bhnv', scores, vi)
        cross = jnp.einsum('bhnd,bhdv->bhnv', qi, S) * q_decay[None, :, :, None]
        S_new = S * c_decay[None, :, None, None] + jnp.einsum('bhmd,bhmv->bhdv', ki * k_decay[None, :, :, None], vi)
        return S_new, inner + cross

    s_fin, out = lax.scan(step, s0.astype(f32), (qc, kc, vc))
    out = out.transpose(1, 0, 3, 2, 4).reshape(B_, T, H, dv).astype(v.dtype)
    return out, s_fin


def context_states(k, v, lg_f, lg_b):
    L = k.shape[1]
    f32 = jnp.float32
    pos = jnp.arange(L, dtype=f32)
    wf = jnp.exp(lg_f.astype(f32)[:, None] * (L - 1.0 - pos))
    wb = jnp.exp(lg_b.astype(f32)[:, None] * pos)
    k32, v32 = k.astype(f32), v.astype(f32)
    sf = jnp.einsum('hl,blhd,blhv->bhdv', wf, k32, v32)
    sb = jnp.einsum('hl,blhd,blhv->bhdv', wb, k32, v32)
    return sf, sb


def mix_tokens(p, s0_f, s0_b, rope, a_ln_g, a_ws, a_bs, a_out_g, lg_f, lg_b, gn_g, w_out):
    B_, T, _ = p.shape
    u, va, q, k, vr, gate = jnp.split(p, SPLITS, axis=-1)
    a = rmsnorm(chunk_mlp(u, va, a_ln_g, a_ws, a_bs), a_out_g)
    q = q.reshape(B_, T, R_HEADS, R_DK)
    k = k.reshape(B_, T, R_HEADS, R_DK)
    vr = vr.reshape(B_, T, R_HEADS, R_DV)
    if rope is not None:
        q = apply_rope(q, rope[0], rope[1])
        k = apply_rope(k, rope[0], rope[1])
    k = k * (R_DK ** -0.5)
    o_f, s_f = retention_scan(q, k, vr, lg_f, s0_f)
    o_b, s_b = retention_scan(q[:, ::-1], k[:, ::-1], vr[:, ::-1], lg_b, s0_b)
    o = o_f + o_b[:, ::-1]
    r = head_groupnorm(o, gn_g) * jax.nn.silu(gate)
    return jnp.concatenate([a, r], axis=-1) @ w_out, s_f, s_b


def moe(h, router_w, router_bias, w1, w3, w2, ws1, ws3, ws2):
    B_, T, D = h.shape
    xt = h.reshape(B_ * T, D)
    N = xt.shape[0]
    s = jax.nn.sigmoid((xt @ router_w).astype(jnp.float32))
    sel = s + router_bias.astype(jnp.float32)
    grp = sel.reshape(N, N_GROUPS, N_EXPERTS // N_GROUPS)
    grp_score = lax.top_k(grp, 2)[0].sum(-1)
    _, gidx = lax.top_k(grp_score, TOPK_GROUPS)
    gmask = jax.nn.one_hot(gidx, N_GROUPS, dtype=jnp.float32).sum(1)
    emask = jnp.repeat(gmask, N_EXPERTS // N_GROUPS, axis=1)
    _, eidx = lax.top_k(jnp.where(emask > 0, sel, -jnp.inf), TOP_K)
    w = jnp.take_along_axis(s, eidx, axis=-1)
    w = w / jnp.sum(w, axis=-1, keepdims=True) * ROUTED_SCALE
    NK = N * TOP_K
    e_flat = eidx.reshape(-1)
    tok_flat = jnp.repeat(jnp.arange(N, dtype=jnp.int32), TOP_K)
    w_flat = w.reshape(-1)
    order = jnp.argsort(e_flat)
    se, st, sw = e_flat[order], tok_flat[order], w_flat[order]
    counts = jnp.zeros((N_EXPERTS,), jnp.int32).at[e_flat].add(1)
    start = jnp.cumsum(counts) - counts
    padded = (counts + MOE_BLOCK - 1) // MOE_BLOCK * MOE_BLOCK
    pend = jnp.cumsum(padded)
    pstart = pend - padded
    dest = pstart[se] + (jnp.arange(NK, dtype=jnp.int32) - start[se])
    n_blocks = -(-NK // MOE_BLOCK) + N_EXPERTS
    P = n_blocks * MOE_BLOCK
    tok_buf = jnp.full((P,), N, jnp.int32).at[dest].set(st)
    w_buf = jnp.zeros((P,), jnp.float32).at[dest].set(sw)
    blk_expert = jnp.minimum(jnp.searchsorted(pend, jnp.arange(n_blocks, dtype=jnp.int32) * MOE_BLOCK, side='right'),
                             N_EXPERTS - 1)
    xpad = jnp.concatenate([xt, jnp.zeros((1, D), xt.dtype)], axis=0)

    def run_block(args):
        tok, ex = args
        xb = xpad[tok]
        hid = jax.nn.silu(xb @ w1[ex]) * (xb @ w3[ex])
        return hid @ w2[ex]

    yb = lax.map(run_block, (tok_buf.reshape(n_blocks, MOE_BLOCK), blk_expert))
    contrib = (yb.reshape(P, D) * w_buf[:, None]).astype(xt.dtype)
    routed = jnp.zeros((N + 1, D), xt.dtype).at[tok_buf].add(contrib)[:N]
    shared = (jax.nn.silu(xt @ ws1) * (xt @ ws3)) @ ws2
    return (routed + shared).reshape(B_, T, D)


def setup_inputs(seed: int = 0) -> dict:
    key = jax.random.key(seed)
    ks = jax.random.split(key, 32)
    f32 = jnp.float32
    D = D_MODEL
    nrm = lambda k, shape, scale: jax.random.normal(k, shape, f32) * scale
    base_lg = jnp.log(1.0 - 2.0 ** (-5.0 - jnp.arange(R_HEADS, dtype=f32)))
    return {
        'x': nrm(ks[0], (BATCH, SEQ, D), 1.0),
        'c': nrm(ks[1], (BATCH, D), 1.0),
        'ctx': nrm(ks[2], (BATCH, CTX_LEN, D), 1.0),
        'c_ctx': nrm(ks[3], (D,), 1.0),
        'w_mod': nrm(ks[4], (DEPTH, D, 6 * D), 0.5 * D ** -0.5),
        'b_mod': nrm(ks[5], (DEPTH, 6 * D), 0.02),
        'norm1_g': 1.0 + nrm(ks[6], (DEPTH, D), 0.02),
        'norm2_g': 1.0 + nrm(ks[7], (DEPTH, D), 0.02),
        'w_in': nrm(ks[8], (DEPTH, D, IN_COLS), D ** -0.5),
        'a_ln_g': 1.0 + nrm(ks[9], (DEPTH, A_WIDTH), 0.02),
        'a_ws': nrm(ks[10], (DEPTH, A_HEADS, CHUNK, CHUNK), CHUNK ** -0.5),
        'a_bs': 1.0 + nrm(ks[11], (DEPTH, A_HEADS, CHUNK), 0.02),
        'a_out_g': 1.0 + nrm(ks[12], (DEPTH, A_WIDTH), 0.02),
        'ret_log_decay_f': base_lg[None] * (1.0 + nrm(ks[13], (DEPTH, R_HEADS), 0.05)),
        'ret_log_decay_b': base_lg[None] * (1.0 + nrm(ks[14], (DEPTH, R_HEADS), 0.05)),
        'ret_gn_g': 1.0 + nrm(ks[15], (DEPTH, R_WIDTH), 0.02),
        'w_out': nrm(ks[16], (DEPTH, MIX_WIDTH, D), MIX_WIDTH ** -0.5),
        'router_w': nrm(ks[17], (DEPTH, D, N_EXPERTS), D ** -0.5),
        'router_bias': nrm(ks[18], (DEPTH, N_EXPERTS), 0.01),
        'exp_w1': nrm(ks[19], (DEPTH, N_EXPERTS, D, EXPERT_HIDDEN), D ** -0.5),
        'exp_w3': nrm(ks[20], (DEPTH, N_EXPERTS, D, EXPERT_HIDDEN), D ** -0.5),
        'exp_w2': nrm(ks[21], (DEPTH, N_EXPERTS, EXPERT_HIDDEN, D), EXPERT_HIDDEN ** -0.5),
        'sh_w1': nrm(ks[22], (DEPTH, D, SHARED_HIDDEN), D ** -0.5),
        'sh_w3': nrm(ks[23], (DEPTH, D, SHARED_HIDDEN), D ** -0.5),
        'sh_w2': nrm(ks[24], (DEPTH, SHARED_HIDDEN, D), SHARED_HIDDEN ** -0.5),
        'final_g': 1.0 + nrm(ks[25], (D,), 0.02),
    }


def reference(x, c, ctx, c_ctx, w_mod, b_mod, norm1_g, norm2_g, w_in, a_ln_g, a_ws, a_bs, a_out_g,
              ret_log_decay_f, ret_log_decay_b, ret_gn_g, w_out, router_w, router_bias,
              exp_w1, exp_w3, exp_w2, sh_w1, sh_w3, sh_w2, final_g):
    B_, T, D = x.shape
    ROWS = T // GRID_W
    row = jnp.repeat(jnp.arange(ROWS, dtype=jnp.float32), GRID_W)
    col = jnp.tile(jnp.arange(GRID_W, dtype=jnp.float32), ROWS)
    n_freq = R_DK // 4
    inv = ROPE_BASE ** (-jnp.arange(n_freq, dtype=jnp.float32) / n_freq)
    ang = jnp.concatenate([row[:, None] * inv, col[:, None] * inv], axis=-1)
    rope = (jnp.cos(ang)[None, :, None, :], jnp.sin(ang)[None, :, None, :])
    kv_lo, kv_hi = SPLITS[2], SPLITS[4]
    for l in range(DEPTH):
        last = l == DEPTH - 1
        mod_x = (jax.nn.silu(c) @ w_mod[l] + b_mod[l])[:, None, :]
        mod_c = (jax.nn.silu(c_ctx) @ w_mod[l] + b_mod[l])[None, None, :]
        sh1x, sc1x, g1x, sh2x, sc2x, g2x = jnp.split(mod_x, 6, axis=-1)
        sh1c, sc1c, g1c, sh2c, sc2c, g2c = jnp.split(mod_c, 6, axis=-1)
        mix_args = (a_ln_g[l], a_ws[l], a_bs[l], a_out_g[l], ret_log_decay_f[l], ret_log_decay_b[l],
                    ret_gn_g[l], w_out[l])
        hc = modulate(rmsnorm(ctx, norm1_g[l]), sh1c, sc1c)
        if last:
            pkv = hc @ w_in[l][:, kv_lo:kv_hi]
            kc_, vc_ = jnp.split(pkv, 2, axis=-1)
            kc_ = kc_.reshape(B_, CTX_LEN, R_HEADS, R_DK) * (R_DK ** -0.5)
            vc_ = vc_.reshape(B_, CTX_LEN, R_HEADS, R_DV)
            s0_f, s0_b = context_states(kc_, vc_, ret_log_decay_f[l], ret_log_decay_b[l])
        else:
            zero_s = jnp.zeros((B_, R_HEADS, R_DK, R_DV), jnp.float32)
            out_c, s0_f, s0_b = mix_tokens(hc @ w_in[l], zero_s, zero_s, None, *mix_args)
        hx = modulate(rmsnorm(x, norm1_g[l]), sh1x, sc1x)
        out_x, _, _ = mix_tokens(hx @ w_in[l], s0_f, s0_b, rope, *mix_args)
        x = x + g1x * out_x
        moe_args = (router_w[l], router_bias[l], exp_w1[l], exp_w3[l], exp_w2[l], sh_w1[l], sh_w3[l], sh_w2[l])
        hx2 = modulate(rmsnorm(x, norm2_g[l]), sh2x, sc2x)
        if last:
            x = x + g2x * moe(hx2, *moe_args)
        else:
            ctx = ctx + g1c * out_c
            hc2 = modulate(rmsnorm(ctx, norm2_g[l]), sh2c, sc2c)
            y = moe(jnp.concatenate([hc2, hx2], axis=1), *moe_args)
            ctx = ctx + g2c * y[:, :CTX_LEN]
            x = x + g2x * y[:, CTX_LEN:]
    return rmsnorm(x, final_g)
```

```python
import functools

import jax
import jax.numpy as jnp
from jax import lax
from jax.experimental import pallas as pl
from jax.experimental.pallas import tpu as pltpu

F32 = jnp.float32
BF16 = jnp.bfloat16

GRID_W = 64
CHUNK = 128
N_HEADS = 8
HEAD_DIM = 128
WIDTH = N_HEADS * HEAD_DIM
ROPE_BASE = 10000.0
N_EXPERTS = 64
TOP_K = 8
N_GROUPS = 8
GROUP_SIZE = N_EXPERTS // N_GROUPS
TOPK_GROUPS = 4
ROUTED_SCALE = 2.5
MOE_BLOCK = 128
EPS = 1e-6

VMEM_LIMIT = 56 * 1024 * 1024


def _sigmoid(x):
    return 1.0 / (1.0 + jnp.exp(-x))


def _silu(x):
    return x * _sigmoid(x)


def _gelu_tanh(x):
    c = 0.7978845608028654
    return 0.5 * x * (1.0 + jnp.tanh(c * (x + 0.044715 * (x * x * x))))


def _rms(x, g):
    return x * lax.rsqrt(jnp.mean(x * x, axis=-1, keepdims=True) + EPS) * g


def _dot(a, b):
    return jnp.dot(a, b, preferred_element_type=F32)


def _dot_t0(a, b):
    return lax.dot_general(a, b, (((0,), (0,)), ((), ())), preferred_element_type=F32)


def _dot_t1(a, b):
    return lax.dot_general(a, b, (((1,), (1,)), ((), ())), preferred_element_type=F32)


def _mod_kernel(cc_ref, w_ref, b_ref, o_ref):
    a = _silu(cc_ref[...]).astype(BF16)
    o_ref[...] = _dot(a, w_ref[...].astype(BF16)) + b_ref[...]


def _modulation(cc, w_mod, b_mod, tn=1024):
    d, n = w_mod.shape
    return pl.pallas_call(
        _mod_kernel,
        grid=(n // tn,),
        in_specs=[pl.BlockSpec((8, d), lambda j: (0, 0)),
                  pl.BlockSpec((d, tn), lambda j: (0, j)),
                  pl.BlockSpec((1, tn), lambda j: (0, j))],
        out_specs=pl.BlockSpec((8, tn), lambda j: (0, j)),
        out_shape=jax.ShapeDtypeStruct((8, n), F32),
        compiler_params=pltpu.CompilerParams(dimension_semantics=("arbitrary",),
                                             vmem_limit_bytes=VMEM_LIMIT),
        name="mod",
    )(cc, w_mod, b_mod.reshape(1, n))


def _ctx_kernel(lgf_ref, lgb_ref, ctx_ref, g_ref, mod_ref, wk_ref, wv_ref, sf_ref, sb_ref, *, mod_row):
    h = pl.program_id(1)
    d = ctx_ref.shape[-1]
    length = ctx_ref.shape[1]
    shift = mod_ref[mod_row:mod_row + 1, 0:d]
    scale = mod_ref[mod_row:mod_row + 1, d:2 * d]
    hc = (_rms(ctx_ref[0], g_ref[...]) * (1.0 + scale) + shift).astype(BF16)
    k = _dot(hc, wk_ref[...]) * (HEAD_DIM ** -0.5)
    v = _dot(hc, wv_ref[...]).astype(BF16)
    pos = lax.broadcasted_iota(jnp.int32, (length, 1), 0).astype(F32)
    wf = jnp.exp(lgf_ref[h] * (length - 1.0 - pos))
    wb = jnp.exp(lgb_ref[h] * pos)
    sf_ref[0, 0] = _dot_t0((k * wf).astype(BF16), v)
    sb_ref[0, 0] = _dot_t0((k * wb).astype(BF16), v)


def _context_states(ctx, norm1_g, mod, mod_row, w_in_b, lg_f, lg_b):
    b, length, d = ctx.shape
    kblk = 3 * N_HEADS
    vblk = 4 * N_HEADS
    smem = pl.BlockSpec(memory_space=pltpu.SMEM)
    out = jax.ShapeDtypeStruct((b, N_HEADS, HEAD_DIM, HEAD_DIM), F32)
    return pl.pallas_call(
        functools.partial(_ctx_kernel, mod_row=mod_row),
        grid=(b, N_HEADS),
        in_specs=[smem, smem,
                  pl.BlockSpec((1, length, d), lambda i, h: (i, 0, 0)),
                  pl.BlockSpec((1, d), lambda i, h: (0, 0)),
                  pl.BlockSpec(mod.shape, lambda i, h: (0, 0)),
                  pl.BlockSpec((d, HEAD_DIM), lambda i, h: (0, kblk + h)),
                  pl.BlockSpec((d, HEAD_DIM), lambda i, h: (0, vblk + h))],
        out_specs=[pl.BlockSpec((1, 1, HEAD_DIM, HEAD_DIM), lambda i, h: (i, h, 0, 0)),
                   pl.BlockSpec((1, 1, HEAD_DIM, HEAD_DIM), lambda i, h: (i, h, 0, 0))],
        out_shape=[out, out],
        compiler_params=pltpu.CompilerParams(dimension_semantics=("arbitrary", "arbitrary"),
                                             vmem_limit_bytes=VMEM_LIMIT),
        name="ctx_states",
    )(lg_f, lg_b, ctx, norm1_g.reshape(1, d), mod, w_in_b, w_in_b)


def _inproj_kernel(x_ref, g_ref, mod_ref, w_ref, lng_ref, ws_ref, bst_ref, aog_ref, cos_ref, sin_ref,
                   a_ref, q_ref, k_ref, v_ref, sg_ref, s_scr):
    tm, d = x_ref.shape
    mod = mod_ref[0]
    hx = (_rms(x_ref[...], g_ref[...]) * (1.0 + mod[:, d:2 * d]) + mod[:, 0:d]).astype(BF16)

    puv = _dot(hx, w_ref[:, 0:2 * WIDTH])
    u = _gelu_tanh(puv[:, :WIDTH])
    v = _gelu_tanh(puv[:, WIDTH:])
    mu = jnp.mean(v, axis=-1, keepdims=True)
    vc = v - mu
    var = jnp.mean(vc * vc, axis=-1, keepdims=True)
    vn = (vc * lax.rsqrt(var + EPS) * lng_ref[...]).astype(BF16)
    for c in range(tm // CHUNK):
        rows = slice(c * CHUNK, (c + 1) * CHUNK)
        for h in range(N_HEADS):
            cols = slice(h * HEAD_DIM, (h + 1) * HEAD_DIM)
            s_scr[rows, cols] = _dot(ws_ref[h], vn[rows, cols]) + bst_ref[:, h:h + 1]
    a_ref[...] = _rms(u * s_scr[...], aog_ref[...]).astype(a_ref.dtype)

    pqk = _dot(hx, w_ref[:, 2 * WIDTH:4 * WIDTH])
    cos2 = cos_ref[...]
    sin2 = sin_ref[...]
    for h in range(N_HEADS):
        cols = slice(h * HEAD_DIM, (h + 1) * HEAD_DIM)
        qh = pqk[:, h * HEAD_DIM:(h + 1) * HEAD_DIM]
        kh = pqk[:, WIDTH + h * HEAD_DIM:WIDTH + (h + 1) * HEAD_DIM]
        q_ref[:, cols] = (qh * cos2 + pltpu.roll(qh, HEAD_DIM // 2, 1) * sin2).astype(q_ref.dtype)
        kr = kh * cos2 + pltpu.roll(kh, HEAD_DIM // 2, 1) * sin2
        k_ref[:, cols] = (kr * (HEAD_DIM ** -0.5)).astype(k_ref.dtype)
    pvg = _dot(hx, w_ref[:, 4 * WIDTH:6 * WIDTH])
    v_ref[...] = pvg[:, :WIDTH].astype(v_ref.dtype)
    sg_ref[...] = _silu(pvg[:, WIDTH:]).astype(sg_ref.dtype)


def _in_projection(x2, norm1_g, mod3, w_in_b, a_ln_g, a_ws_b, a_bs_t, a_out_g, cos2, sin2, seq, tm=256):
    n, d = x2.shape
    tiles_per_seq = seq // tm
    vmem = pl.BlockSpec(memory_space=pltpu.VMEM)
    row = lambda i: (i, 0)
    const = lambda i: (0, 0)
    out = jax.ShapeDtypeStruct((n, WIDTH), BF16)
    return pl.pallas_call(
        _inproj_kernel,
        grid=(n // tm,),
        in_specs=[pl.BlockSpec((tm, d), row),
                  pl.BlockSpec((1, d), const),
                  pl.BlockSpec((1, 1, mod3.shape[-1]), lambda i: (i // tiles_per_seq, 0, 0)),
                  vmem,
                  pl.BlockSpec((1, WIDTH), const),
                  vmem,
                  pl.BlockSpec((CHUNK, N_HEADS), const),
                  pl.BlockSpec((1, WIDTH), const),
                  pl.BlockSpec((tm, HEAD_DIM), lambda i: (i % tiles_per_seq, 0)),
                  pl.BlockSpec((tm, HEAD_DIM), lambda i: (i % tiles_per_seq, 0))],
        out_specs=[pl.BlockSpec((tm, WIDTH), row)] * 5,
        out_shape=[out] * 5,
        scratch_shapes=[pltpu.VMEM((tm, WIDTH), F32)],
        compiler_params=pltpu.CompilerParams(dimension_semantics=("arbitrary",),
                                             vmem_limit_bytes=VMEM_LIMIT),
        name="inproj",
    )(x2, norm1_g.reshape(1, d), mod3, w_in_b, a_ln_g.reshape(1, WIDTH), a_ws_b, a_bs_t,
      a_out_g.reshape(1, WIDTH), cos2, sin2)


def _ret_kernel(lgf_ref, lgb_ref, q_ref, k_ref, v_ref, sg_ref, s0f_ref, s0b_ref, gng_ref, r_ref, sb_scr):
    h = pl.program_id(1)
    nc = q_ref.shape[0] // CHUNK
    lgf = lgf_ref[h]
    lgb = lgb_ref[h]
    c_f = float(CHUNK)
    pos = lax.broadcasted_iota(jnp.int32, (CHUNK, 1), 0).astype(F32)
    kdec_f = jnp.exp(lgf * (c_f - 1.0 - pos))
    kdec_b = jnp.exp(lgb * pos)
    qdec_f = jnp.exp(lgf * (pos + 1.0))
    qdec_b = jnp.exp(lgb * (c_f - pos))
    cdec_f = jnp.exp(jnp.full((1, HEAD_DIM), lgf * c_f, F32))
    cdec_b = jnp.exp(jnp.full((1, HEAD_DIM), lgb * c_f, F32))
    diff = (lax.broadcasted_iota(jnp.int32, (CHUNK, CHUNK), 0)
            - lax.broadcasted_iota(jnp.int32, (CHUNK, CHUNK), 1)).astype(F32)
    d_f = jnp.where(diff >= 0, jnp.exp(lgf * jnp.maximum(diff, 0.0)), 0.0)
    d_b = jnp.where(diff <= 0, jnp.exp(lgb * jnp.maximum(-diff, 0.0)), 0.0)
    d_tot = d_f + d_b
    gng = gng_ref[...]

    def chunk(ref, c):
        return ref[pl.ds(pl.multiple_of(c * CHUNK, CHUNK), CHUNK), :]

    def bwd(j, s):
        c = nc - 1 - j
        sb_scr[c] = s.astype(BF16)
        kd = (chunk(k_ref, c).astype(F32) * kdec_b).astype(BF16)
        return s * cdec_b + _dot_t0(kd, chunk(v_ref, c))

    lax.fori_loop(0, nc, bwd, s0b_ref[0, 0])

    def fwd(c, s):
        q = chunk(q_ref, c)
        k = chunk(k_ref, c)
        v = chunk(v_ref, c)
        scores = _dot_t1(q, k) * d_tot
        o = _dot(scores.astype(BF16), v)
        o = o + _dot(q, s.astype(BF16)) * qdec_f
        o = o + _dot(q, sb_scr[c]) * qdec_b
        mu = jnp.mean(o, axis=-1, keepdims=True)
        oc = o - mu
        var = jnp.mean(oc * oc, axis=-1, keepdims=True)
        y = oc * lax.rsqrt(var + EPS) * gng * chunk(sg_ref, c).astype(F32)
        r_ref[pl.ds(pl.multiple_of(c * CHUNK, CHUNK), CHUNK), :] = y.astype(r_ref.dtype)
        kd = (k.astype(F32) * kdec_f).astype(BF16)
        return s * cdec_f + _dot_t0(kd, v)

    lax.fori_loop(0, nc, fwd, s0f_ref[0, 0])


def _retention(q, k, v, sg, s0f, s0b, gn_g, lg_f, lg_b, batch, seq):
    smem = pl.BlockSpec(memory_space=pltpu.SMEM)
    head = pl.BlockSpec((seq, HEAD_DIM), lambda b, h: (b, h))
    state = pl.BlockSpec((1, 1, HEAD_DIM, HEAD_DIM), lambda b, h: (b, h, 0, 0))
    return pl.pallas_call(
        _ret_kernel,
        grid=(batch, N_HEADS),
        in_specs=[smem, smem, head, head, head, head, state, state,
                  pl.BlockSpec((1, HEAD_DIM), lambda b, h: (0, h))],
        out_specs=head,
        out_shape=jax.ShapeDtypeStruct(q.shape, BF16),
        scratch_shapes=[pltpu.VMEM((seq // CHUNK, HEAD_DIM, HEAD_DIM), BF16)],
        compiler_params=pltpu.CompilerParams(dimension_semantics=("arbitrary", "arbitrary"),
                                             vmem_limit_bytes=VMEM_LIMIT),
        name="retention",
    )(lg_f, lg_b, q, k, v, sg, s0f, s0b, gn_g.reshape(1, WIDTH))


def _outproj_kernel(a_ref, r_ref, x_ref, mod_ref, wo_ref, g2_ref, rw_ref, rb_ref,
                    x1_ref, hx2_ref, eidx_ref, wgt_ref, rank_ref, cnt_ref, carry):
    i = pl.program_id(0)
    tm, d = x_ref.shape
    mod = mod_ref[0]
    y = _dot(a_ref[...], wo_ref[0:WIDTH, :]) + _dot(r_ref[...], wo_ref[WIDTH:2 * WIDTH, :])
    x1 = x_ref[...] + mod[:, 2 * d:3 * d] * y
    x1_ref[...] = x1
    hx2 = _rms(x1, g2_ref[...]) * (1.0 + mod[:, 4 * d:5 * d]) + mod[:, 3 * d:4 * d]
    hx2_ref[...] = hx2

    s = _sigmoid(_dot(hx2.astype(BF16), rw_ref[...]))
    sel = s + rb_ref[...]
    lane_i = lax.broadcasted_iota(jnp.int32, (tm, N_EXPERTS), 1)
    lane = lane_i.astype(F32)
    grp = lane_i // GROUP_SIZE
    no_lane = float(N_EXPERTS)
    neg = -jnp.inf
    gscore = jnp.zeros((tm, N_EXPERTS), F32)
    gcols = []
    for g in range(N_GROUPS):
        in_g = grp == g
        m1 = jnp.max(jnp.where(in_g, sel, neg), axis=-1, keepdims=True)
        i1 = jnp.min(jnp.where(in_g & (sel == m1), lane, no_lane), axis=-1, keepdims=True)
        m2 = jnp.max(jnp.where(in_g & (lane != i1), sel, neg), axis=-1, keepdims=True)
        gcols.append(m1 + m2)
        gscore = jnp.where(in_g, m1 + m2, gscore)
    grank = jnp.zeros((tm, N_EXPERTS), jnp.int32)
    for g in range(N_GROUPS):
        beats = (gcols[g] > gscore) | ((gcols[g] == gscore) & (g < grp))
        grank = grank + beats.astype(jnp.int32)
    masked = jnp.where(grank < TOPK_GROUPS, sel, neg)
    chosen_f = jnp.zeros((tm, N_EXPERTS), F32)
    idx_cols = []
    for _ in range(TOP_K):
        m = jnp.max(masked, axis=-1, keepdims=True)
        idx = jnp.min(jnp.where(masked == m, lane, no_lane), axis=-1, keepdims=True)
        hit = lane == idx
        idx_cols.append(idx)
        chosen_f = chosen_f + jnp.where(hit, 1.0, 0.0)
        masked = jnp.where(hit, neg, masked)
    denom = jnp.sum(chosen_f * s, axis=-1, keepdims=True)

    @pl.when(i == 0)
    def _():
        carry[...] = jnp.zeros_like(carry)

    row_i = lax.broadcasted_iota(jnp.int32, (tm, tm), 0)
    col_i = lax.broadcasted_iota(jnp.int32, (tm, tm), 1)
    tri = jnp.where(col_i < row_i, 1.0, 0.0).astype(BF16)
    before = _dot(tri, chosen_f.astype(BF16)) + carry[...]
    carry[...] = carry[...] + jnp.sum(chosen_f, axis=0, keepdims=True)
    cnt_ref[...] = carry[...].astype(jnp.int32)

    for kk in range(TOP_K):
        hit = lane == idx_cols[kk]
        eidx_ref[:, kk:kk + 1] = idx_cols[kk].astype(jnp.int32)
        wk = jnp.sum(jnp.where(hit, s, 0.0), axis=-1, keepdims=True)
        wgt_ref[:, kk:kk + 1] = wk / denom * ROUTED_SCALE
        rk = jnp.sum(jnp.where(hit, before, 0.0), axis=-1, keepdims=True)
        rank_ref[:, kk:kk + 1] = rk.astype(jnp.int32)


def _out_projection(a, r, x2, mod3, w_out_b, norm2_g, router_w_b, router_bias, seq, tm=256):
    n, d = x2.shape
    tiles_per_seq = seq // tm
    vmem = pl.BlockSpec(memory_space=pltpu.VMEM)
    row = lambda i: (i, 0)
    const = lambda i: (0, 0)
    return pl.pallas_call(
        _outproj_kernel,
        grid=(n // tm,),
        in_specs=[pl.BlockSpec((tm, WIDTH), row),
                  pl.BlockSpec((tm, WIDTH), row),
                  pl.BlockSpec((tm, d), row),
                  pl.BlockSpec((1, 1, mod3.shape[-1]), lambda i: (i // tiles_per_seq, 0, 0)),
                  vmem,
                  pl.BlockSpec((1, d), const),
                  vmem,
                  pl.BlockSpec((1, N_EXPERTS), const)],
        out_specs=[pl.BlockSpec((tm, d), row),
                   pl.BlockSpec((tm, d), row),
                   pl.BlockSpec((tm, TOP_K), row),
                   pl.BlockSpec((tm, TOP_K), row),
                   pl.BlockSpec((tm, TOP_K), row),
                   pl.BlockSpec((1, N_EXPERTS), const)],
        out_shape=[jax.ShapeDtypeStruct((n, d), F32),
                   jax.ShapeDtypeStruct((n, d), F32),
                   jax.ShapeDtypeStruct((n, TOP_K), jnp.int32),
                   jax.ShapeDtypeStruct((n, TOP_K), F32),
                   jax.ShapeDtypeStruct((n, TOP_K), jnp.int32),
                   jax.ShapeDtypeStruct((1, N_EXPERTS), jnp.int32)],
        scratch_shapes=[pltpu.VMEM((1, N_EXPERTS), F32)],
        compiler_params=pltpu.CompilerParams(dimension_semantics=("arbitrary",),
                                             vmem_limit_bytes=VMEM_LIMIT),
        name="outproj_router",
    )(a, r, x2, mod3, w_out_b, norm2_g.reshape(1, d), router_w_b, router_bias.reshape(1, N_EXPERTS))


def _moe_kernel(be_ref, bv_ref, nused_ref, ord_cur, ord_nxt, x_hbm, w1_ref, w3_ref, w2_ref, y_hbm,
                xbuf, ybuf, w1b, w3b, w2b, gsem, ssem):
    i = pl.program_id(0)
    n_steps = pl.num_programs(0)
    n_used = nused_ref[0]
    n_tok = x_hbm.shape[0]
    slot = i % 2

    def gather_copy(tok, j, s):
        return pltpu.make_async_copy(x_hbm.at[pl.ds(tok, 1), :], xbuf.at[s, pl.ds(j, 1), :], gsem.at[s])

    def scatter_copy(dst_row, j, s):
        return pltpu.make_async_copy(ybuf.at[s, pl.ds(j, 1), :], y_hbm.at[pl.ds(dst_row, 1), :], ssem.at[s])

    def start_gather(ord_ref, s):
        def body(j, carry):
            gather_copy(ord_ref[0, 0, j] & (n_tok - 1), j, s).start()
            return carry
        lax.fori_loop(0, MOE_BLOCK, body, 0, unroll=8)

    def wait_gather(s):
        pltpu.make_async_copy(x_hbm.at[pl.ds(0, MOE_BLOCK), :], xbuf.at[s], gsem.at[s]).wait()

    def start_scatter(blk, s):
        def body(j, carry):
            scatter_copy(ord_cur[0, 0, j], j, s).start()
            return carry

        @pl.when(bv_ref[blk] == MOE_BLOCK)
        def _():
            lax.fori_loop(0, MOE_BLOCK, body, 0, unroll=8)

        @pl.when(bv_ref[blk] < MOE_BLOCK)
        def _():
            lax.fori_loop(0, bv_ref[blk], body, 0)

    def wait_scatter(blk, s):
        @pl.when(bv_ref[blk] == MOE_BLOCK)
        def _():
            pltpu.make_async_copy(ybuf.at[s], y_hbm.at[pl.ds(0, MOE_BLOCK), :], ssem.at[s]).wait()

        @pl.when(bv_ref[blk] < MOE_BLOCK)
        def _():
            def body(j, carry):
                scatter_copy(0, 0, s).wait()
                return carry
            lax.fori_loop(0, bv_ref[blk], body, 0)

    @pl.when((i == 0) & (n_used > 0))
    def _():
        start_gather(ord_cur, 0)

    @pl.when(i + 1 < n_used)
    def _():
        start_gather(ord_nxt, 1 - slot)

    @pl.when(i < n_used)
    def _():
        prev = be_ref[jnp.maximum(i - 1, 0)]

        @pl.when((i == 0) | (be_ref[i] != prev))
        def _():
            w1b[...] = w1_ref[0].astype(BF16)
            w3b[...] = w3_ref[0].astype(BF16)
            w2b[...] = w2_ref[0].astype(BF16)

        wait_gather(slot)

        @pl.when(i >= 2)
        def _():
            wait_scatter(i - 2, slot)

        xb = xbuf[slot].astype(BF16)
        hid = (_silu(_dot(xb, w1b[...])) * _dot(xb, w3b[...])).astype(BF16)
        ybuf[slot] = _dot(hid, w2b[...])
        start_scatter(i, slot)

    @pl.when(i == n_steps - 1)
    def _():
        @pl.when(n_used >= 2)
        def _():
            wait_scatter(n_used - 2, n_used % 2)

        @pl.when(n_used >= 1)
        def _():
            wait_scatter(n_used - 1, (n_used - 1) % 2)


def _routed_experts(hx2, order3, blk_expert, blk_valid, n_used, w1, w3, w2):
    n, d = hx2.shape
    n_blocks = order3.shape[0]
    eh = w1.shape[-1]
    any_spec = pl.BlockSpec(memory_space=pl.ANY)
    grid_spec = pltpu.PrefetchScalarGridSpec(
        num_scalar_prefetch=3,
        grid=(n_blocks,),
        in_specs=[pl.BlockSpec((1, 1, MOE_BLOCK), lambda i, be, bv, nu: (i, 0, 0), memory_space=pltpu.SMEM),
                  pl.BlockSpec((1, 1, MOE_BLOCK),
                               lambda i, be, bv, nu: (jnp.minimum(i + 1, n_blocks - 1), 0, 0),
                               memory_space=pltpu.SMEM),
                  any_spec,
                  pl.BlockSpec((1, d, eh), lambda i, be, bv, nu: (be[i], 0, 0)),
                  pl.BlockSpec((1, d, eh), lambda i, be, bv, nu: (be[i], 0, 0)),
                  pl.BlockSpec((1, eh, d), lambda i, be, bv, nu: (be[i], 0, 0))],
        out_specs=any_spec,
        scratch_shapes=[pltpu.VMEM((2, MOE_BLOCK, d), F32),
                        pltpu.VMEM((2, MOE_BLOCK, d), F32),
                        pltpu.VMEM((d, eh), BF16),
                        pltpu.VMEM((d, eh), BF16),
                        pltpu.VMEM((eh, d), BF16),
                        pltpu.SemaphoreType.DMA((2,)),
                        pltpu.SemaphoreType.DMA((2,))],
    )
    return pl.pallas_call(
        _moe_kernel,
        grid_spec=grid_spec,
        out_shape=jax.ShapeDtypeStruct((n * TOP_K, d), F32),
        compiler_params=pltpu.CompilerParams(dimension_semantics=("arbitrary",),
                                             vmem_limit_bytes=VMEM_LIMIT),
        name="moe_experts",
    )(blk_expert, blk_valid, n_used, order3, order3, hx2, w1, w3, w2)


def _final_kernel(*refs):
    y_refs = refs[:TOP_K]
    wgt_ref, hx2_ref, x1_ref, mod_ref, ws1_ref, ws3_ref, ws2_ref, fg_ref, o_ref = refs[TOP_K:]
    d = x1_ref.shape[-1]
    mod = mod_ref[0]
    wgt = wgt_ref[...]
    hb = hx2_ref[...].astype(BF16)
    hid = (_silu(_dot(hb, ws1_ref[...])) * _dot(hb, ws3_ref[...])).astype(BF16)
    acc = _dot(hid, ws2_ref[...])
    for kk in range(TOP_K):
        acc = acc + wgt[:, kk:kk + 1] * y_refs[kk][...]
    o_ref[...] = _rms(x1_ref[...] + mod[:, 5 * d:6 * d] * acc, fg_ref[...])


def _final(y_slots, wgt, hx2, x1, mod3, ws1_b, ws3_b, ws2_b, final_g, seq, tm=128):
    n, d = x1.shape
    tiles_per_seq = seq // tm
    tiles = n // tm
    vmem = pl.BlockSpec(memory_space=pltpu.VMEM)
    row = lambda i: (i, 0)
    y_specs = [pl.BlockSpec((tm, d), functools.partial(lambda i, kk: (kk * tiles + i, 0), kk=kk))
               for kk in range(TOP_K)]
    return pl.pallas_call(
        _final_kernel,
        grid=(tiles,),
        in_specs=y_specs + [pl.BlockSpec((tm, TOP_K), row),
                            pl.BlockSpec((tm, d), row),
                            pl.BlockSpec((tm, d), row),
                            pl.BlockSpec((1, 1, mod3.shape[-1]), lambda i: (i // tiles_per_seq, 0, 0)),
                            vmem, vmem, vmem,
                            pl.BlockSpec((1, d), lambda i: (0, 0))],
        out_specs=pl.BlockSpec((tm, d), row),
        out_shape=jax.ShapeDtypeStruct((n, d), F32),
        compiler_params=pltpu.CompilerParams(dimension_semantics=("arbitrary",),
                                             vmem_limit_bytes=VMEM_LIMIT),
        name="shared_combine_final",
    )(*([y_slots] * TOP_K), wgt, hx2, x1, mod3, ws1_b, ws3_b, ws2_b, final_g.reshape(1, d))


def _rope_tables(seq):
    rows = seq // GRID_W
    row = jnp.repeat(jnp.arange(rows, dtype=F32), GRID_W)
    col = jnp.tile(jnp.arange(GRID_W, dtype=F32), rows)
    n_freq = HEAD_DIM // 4
    inv = ROPE_BASE ** (-jnp.arange(n_freq, dtype=F32) / n_freq)
    ang = jnp.concatenate([row[:, None] * inv, col[:, None] * inv], axis=-1)
    cos, sin = jnp.cos(ang), jnp.sin(ang)
    return jnp.concatenate([cos, cos], axis=-1), jnp.concatenate([-sin, sin], axis=-1)


def _dispatch_tables(eidx, rank, counts, n):
    nk = n * TOP_K
    n_blocks = -(-nk // MOE_BLOCK) + N_EXPERTS
    padded = (counts + MOE_BLOCK - 1) // MOE_BLOCK * MOE_BLOCK
    pend = jnp.cumsum(padded)
    pstart = pend - padded
    dest = (pstart[eidx] + rank).T.reshape(-1)
    order = jnp.zeros((n_blocks * MOE_BLOCK,), jnp.int32).at[dest].set(
        jnp.arange(nk, dtype=jnp.int32), unique_indices=True)
    blk_start = jnp.arange(n_blocks, dtype=jnp.int32) * MOE_BLOCK
    blk_expert = jnp.sum((pend[None, :] <= blk_start[:, None]).astype(jnp.int32), axis=1)
    blk_expert = jnp.minimum(blk_expert, N_EXPERTS - 1)
    blk_valid = jnp.clip(pstart[blk_expert] + counts[blk_expert] - blk_start, 0, MOE_BLOCK).astype(jnp.int32)
    n_used = (pend[-1:] // MOE_BLOCK).astype(jnp.int32)
    return order.reshape(n_blocks, 1, MOE_BLOCK), blk_expert, blk_valid, n_used


def kernel(x, c, ctx, c_ctx, w_mod, b_mod, norm1_g, norm2_g, w_in, a_ln_g, a_ws, a_bs, a_out_g,
           ret_log_decay_f, ret_log_decay_b, ret_gn_g, w_out, router_w, router_bias,
           exp_w1, exp_w3, exp_w2, sh_w1, sh_w3, sh_w2, final_g):
    batch, seq, d = x.shape
    n = batch * seq
    assert w_mod.shape[0] == 1 and batch <= 7 and n & (n - 1) == 0
    x2 = x.reshape(n, d)

    cc = jnp.zeros((8, d), F32).at[:batch].set(c).at[batch].set(c_ctx)
    mod = _modulation(cc, w_mod[0], b_mod[0])
    mod3 = mod.reshape(8, 1, 6 * d)

    w_in_b = w_in[0].astype(BF16)
    lg_f, lg_b = ret_log_decay_f[0], ret_log_decay_b[0]
    s0f, s0b = _context_states(ctx, norm1_g[0], mod, batch, w_in_b, lg_f, lg_b)

    cos2, sin2 = _rope_tables(seq)
    a, q, k, v, sg = _in_projection(x2, norm1_g[0], mod3, w_in_b, a_ln_g[0], a_ws[0].astype(BF16),
                                    a_bs[0].T, a_out_g[0], cos2, sin2, seq)
    r = _retention(q, k, v, sg, s0f, s0b, ret_gn_g[0], lg_f, lg_b, batch, seq)

    x1, hx2, eidx, wgt, rank, counts = _out_projection(
        a, r, x2, mod3, w_out[0].astype(BF16), norm2_g[0], router_w[0].astype(BF16), router_bias[0], seq)

    order3, blk_expert, blk_valid, n_used = _dispatch_tables(eidx, rank, counts[0], n)
    y_slots = _routed_experts(hx2, order3, blk_expert, blk_valid, n_used, exp_w1[0], exp_w3[0], exp_w2[0])

    out = _final(y_slots, wgt, hx2, x1, mod3, sh_w1[0].astype(BF16), sh_w3[0].astype(BF16),
                 sh_w2[0].astype(BF16), final_g, seq)
    return out.reshape(batch, seq, d)
```
